```python
import jax, jax.numpy as jnp
from jax import lax
import numpy as np

D_MODEL = 1024
BATCH = 4
SEQ = 8192
DEPTH = 1

CHUNK = 64
CONV_DIM = D_MODEL
CONV_WIDTH = 31
GLA_HEADS = 4
GLA_KEY_DIM = D_MODEL // 2
GLA_VAL_DIM = D_MODEL
GLA_HEAD_K = GLA_KEY_DIM // GLA_HEADS
GLA_HEAD_V = GLA_VAL_DIM // GLA_HEADS
GATE_RANK = 16
GATE_TEMP = 16.0
N_BRANCHES = 2
D_FF = 2816
FFN_CONV_WIDTH = 3
EPS = 1e-6

IN_SPLITS = (CONV_DIM, CONV_DIM, GLA_KEY_DIM, GLA_KEY_DIM, GLA_VAL_DIM, GLA_VAL_DIM,
             GATE_RANK, N_BRANCHES * D_MODEL)
IN_DIM = sum(IN_SPLITS)

kernel_name = "hybrid_conformer_gla_convffn"


def rmsnorm(x, g):
    xf = x.astype(jnp.float32)
    y = xf * lax.rsqrt(jnp.mean(xf * xf, axis=-1, keepdims=True) + EPS)
    return (y * g.astype(jnp.float32)).astype(x.dtype)


def layernorm(x, g, b):
    xf = x.astype(jnp.float32)
    mu = jnp.mean(xf, axis=-1, keepdims=True)
    xc = xf - mu
    y = xc * lax.rsqrt(jnp.mean(xc * xc, axis=-1, keepdims=True) + EPS)
    return (y * g.astype(jnp.float32) + b.astype(jnp.float32)).astype(x.dtype)


def causal_dwconv(x, w, b):
    width, ch = w.shape
    y = lax.conv_general_dilated(
        x, w[:, None, :].astype(x.dtype), window_strides=(1,),
        padding=[(width - 1, 0)], dimension_numbers=('NWC', 'WIO', 'NWC'),
        feature_group_count=ch)
    return y + b.astype(x.dtype)


def gla_chunk_causal(q, k, v, log_a):
    bsz, seq, heads, dk = q.shape
    dv = v.shape[-1]
    nc = seq // CHUNK

    def to_chunks(t):
        return t.reshape(bsz, nc, CHUNK, heads, t.shape[-1]).transpose(1, 0, 3, 2, 4)

    qc, kc, vc, ac = (to_chunks(t) for t in (q, k, v, log_a))

    def step(state, inp):
        q_c, k_c, v_c, a_c = inp
        cum = jnp.cumsum(a_c.astype(jnp.float32), axis=-2)
        total = cum[..., -1:, :]
        k_dec = k_c.astype(jnp.float32) * jnp.exp(total - cum)
        state = (state * jnp.exp(total[..., 0, :])[..., None]
                 + jnp.einsum('bhck,bhcv->bhkv', k_dec, v_c.astype(jnp.float32)))
        o_c = jnp.einsum('bhck,bhkv->bhcv', q_c.astype(jnp.float32), state)
        return state, o_c

    s0 = jnp.zeros((bsz, heads, dk, dv), jnp.float32)
    _, o = lax.scan(step, s0, (qc, kc, vc, ac))
    return o.transpose(1, 0, 3, 2, 4).reshape(bsz, seq, heads, dv)


def setup_inputs(seed: int = 0) -> dict:
    key = jax.random.key(seed)
    ks = jax.random.split(key, 24)
    f32 = jnp.float32

    def w(k, shape, fan_in):
        return jax.random.normal(k, shape, f32) * (fan_in ** -0.5)

    def gain(k, shape):
        return 1.0 + 0.02 * jax.random.normal(k, shape, f32)

    def bias(k, shape, s=0.01):
        return s * jax.random.normal(k, shape, f32)

    return {
        "x": jax.random.normal(ks[0], (BATCH, SEQ, D_MODEL), f32),
        "norm_mix": gain(ks[1], (DEPTH, D_MODEL)),
        "w_in": w(ks[2], (DEPTH, D_MODEL, IN_DIM), D_MODEL),
        "b_merge": bias(ks[3], (DEPTH, N_BRANCHES * D_MODEL), 0.1),
        "conv_dw": w(ks[4], (DEPTH, CONV_WIDTH, CONV_DIM), CONV_WIDTH),
        "conv_dw_b": bias(ks[5], (DEPTH, CONV_DIM)),
        "conv_ln_g": gain(ks[6], (DEPTH, CONV_DIM)),
        "conv_ln_b": bias(ks[7], (DEPTH, CONV_DIM)),
        "w_conv_out": w(ks[8], (DEPTH, CONV_DIM, D_MODEL), CONV_DIM),
        "w_gk2": w(ks[9], (DEPTH, GATE_RANK, GLA_KEY_DIM), GATE_RANK),
        "b_gk": bias(ks[10], (DEPTH, GLA_KEY_DIM), 0.1),
        "gla_norm": gain(ks[11], (DEPTH, GLA_HEADS, GLA_HEAD_V)),
        "w_gla_out": w(ks[12], (DEPTH, GLA_VAL_DIM, D_MODEL), GLA_VAL_DIM),
        "w_out": w(ks[13], (DEPTH, D_MODEL, D_MODEL), D_MODEL),
        "norm_ffn": gain(ks[14], (DEPTH, D_MODEL)),
        "w_up": w(ks[15], (DEPTH, D_MODEL, 2 * D_FF), D_MODEL),
        "ffn_dw": w(ks[16], (DEPTH, FFN_CONV_WIDTH, 2 * D_FF), FFN_CONV_WIDTH),
        "ffn_dw_b": bias(ks[17], (DEPTH, 2 * D_FF)),
        "w_down": w(ks[18], (DEPTH, D_FF, D_MODEL), D_FF),
        "norm_final": gain(ks[19], (D_MODEL,)),
    }


def reference(x, norm_mix, w_in, b_merge, conv_dw, conv_dw_b, conv_ln_g, conv_ln_b,
              w_conv_out, w_gk2, b_gk, gla_norm, w_gla_out, w_out, norm_ffn, w_up,
              ffn_dw, ffn_dw_b, w_down, norm_final):
    bsz, seq, _ = x.shape
    split_idx = [int(i) for i in np.cumsum(IN_SPLITS)[:-1]]
    for l in range(DEPTH):
        h = rmsnorm(x, norm_mix[l])
        proj = h @ w_in[l].astype(x.dtype)
        c_val, c_gate, q, k, v, g_out, g_low, m_logits = jnp.split(proj, split_idx, axis=-1)

        a = c_val * jax.nn.sigmoid(c_gate)
        a = causal_dwconv(a, conv_dw[l], conv_dw_b[l])
        a = jax.nn.silu(layernorm(a, conv_ln_g[l], conv_ln_b[l]))
        branch_a = a @ w_conv_out[l].astype(x.dtype)

        gk = (g_low @ w_gk2[l].astype(x.dtype) + b_gk[l].astype(x.dtype)).astype(jnp.float32)
        log_a = jax.nn.log_sigmoid(gk) / GATE_TEMP
        qh = q.reshape(bsz, seq, GLA_HEADS, GLA_HEAD_K) * (GLA_HEAD_K ** -0.5)
        kh = k.reshape(bsz, seq, GLA_HEADS, GLA_HEAD_K)
        vh = v.reshape(bsz, seq, GLA_HEADS, GLA_HEAD_V)
        ah = log_a.reshape(bsz, seq, GLA_HEADS, GLA_HEAD_K)
        o = gla_chunk_causal(qh, kh, vh, ah)
        o = rmsnorm(o, gla_norm[l]).astype(x.dtype)
        o = o.reshape(bsz, seq, GLA_VAL_DIM) * jax.nn.silu(g_out)
        branch_b = o @ w_gla_out[l].astype(x.dtype)

        gates = jax.nn.sigmoid(m_logits + b_merge[l].astype(x.dtype))
        g_a, g_b = jnp.split(gates, N_BRANCHES, axis=-1)
        x = x + (g_a * branch_a + g_b * branch_b) @ w_out[l].astype(x.dtype)

        h = rmsnorm(x, norm_ffn[l])
        u = causal_dwconv(h @ w_up[l].astype(x.dtype), ffn_dw[l], ffn_dw_b[l])
        u_gate, u_val = jnp.split(u, 2, axis=-1)
        x = x + (jax.nn.silu(u_gate) * u_val) @ w_down[l].astype(x.dtype)
    return rmsnorm(x, norm_final)
```

```python
import functools

import jax
import jax.numpy as jnp
from jax import lax
from jax.experimental import pallas as pl
from jax.experimental.pallas import tpu as pltpu

CHUNK = 64
GLA_HEADS = 4
GATE_TEMP = 16.0
EPS = 1e-6

SUBLANES = 8
LANES = 128
MXU_DIM = 256

MIX_TT = 256
FFN_TT = 256
CONV_ROWS = 32
MIX_VMEM_BYTES = 52 * 1024 * 1024
FFN_VMEM_BYTES = 52 * 1024 * 1024


def _round_up(n, m):
    return (n + m - 1) // m * m


def _dot(a, b):
    return jnp.dot(a, b, preferred_element_type=jnp.float32)


def _sigmoid(x):
    return 1.0 / (1.0 + jnp.exp(-x))


def _rmsnorm(x, g):
    return x * lax.rsqrt(jnp.mean(x * x, axis=-1, keepdims=True) + EPS) * g


def _mixer_kernel(x_ref, nmix_ref, wglu_ref, wqkvg_ref, wglow_ref, wmerge_ref, bmerge_ref,
                  cdw_ref, cdwb_ref, lng_ref, lnb_ref, wco_ref, wgk2_ref, bgk_ref,
                  gnorm_ref, wgo_ref, wout_ref, tri_ref,
                  o_ref,
                  abuf, ca_s, q_s, k_s, v_s, g_s, cum_s, gate_s, merged_s, ob_s, state,
                  *, tt, halo, conv_w, d_conv, dk_total, dv_total):
    heads = GLA_HEADS
    dk = dk_total // heads
    dv = dv_total // heads
    t = pl.program_id(1)

    @pl.when(t == 0)
    def _():
        abuf[0:halo, :] = jnp.zeros((halo, d_conv), jnp.float32)
        state[...] = jnp.zeros_like(state)

    @pl.when(t > 0)
    def _():
        abuf[0:halo, :] = abuf[tt:tt + halo, :]

    x = x_ref[...]
    hb = _rmsnorm(x, nmix_ref[...]).astype(jnp.bfloat16)

    glu = _dot(hb, wglu_ref[...])
    abuf[halo:halo + tt, :] = glu[:, :d_conv] * _sigmoid(glu[:, d_conv:])

    qkvg = _dot(hb, wqkvg_ref[...])
    q_s[...] = (qkvg[:, :dk_total] * (dk ** -0.5)).astype(jnp.bfloat16)
    k_s[...] = qkvg[:, dk_total:2 * dk_total]
    v_s[...] = qkvg[:, 2 * dk_total:2 * dk_total + dv_total].astype(jnp.bfloat16)
    g_out = qkvg[:, 2 * dk_total + dv_total:]
    g_s[...] = g_out * _sigmoid(g_out)

    g_low = _dot(hb, wglow_ref[...])
    gk = _dot(g_low.astype(jnp.bfloat16), wgk2_ref[...]) + bgk_ref[...]
    log_a = (jnp.minimum(gk, 0.0) - jnp.log1p(jnp.exp(-jnp.abs(gk)))) * (1.0 / GATE_TEMP)
    la_hi = log_a.astype(jnp.bfloat16)
    la_lo = (log_a - la_hi.astype(jnp.float32)).astype(jnp.bfloat16)
    tri = tri_ref[...]
    cum_s[...] = _dot(tri, la_hi) + _dot(tri, la_lo)

    gate_s[...] = _sigmoid(_dot(hb, wmerge_ref[...]) + bmerge_ref[...])

    base = halo - (conv_w - 1)

    for r0 in range(0, tt, CONV_ROWS):
        acc = cdwb_ref[...] + cdw_ref[0:1, :] * abuf[r0 + base:r0 + base + CONV_ROWS, :]
        for j in range(1, conv_w):
            acc = acc + cdw_ref[j:j + 1, :] * abuf[r0 + base + j:r0 + base + j + CONV_ROWS, :]
        mu = jnp.mean(acc, axis=-1, keepdims=True)
        xc = acc - mu
        y = xc * lax.rsqrt(jnp.mean(xc * xc, axis=-1, keepdims=True) + EPS)
        y = y * lng_ref[...] + lnb_ref[...]
        ca_s[r0:r0 + CONV_ROWS, :] = (y * _sigmoid(y)).astype(jnp.bfloat16)
    merged_s[...] = gate_s[:, :d_conv] * _dot(ca_s[...], wco_ref[...])

    def gla_body(c, carry):
        r0 = pl.multiple_of(c * CHUNK, CHUNK)
        cum = cum_s[pl.ds(r0, CHUNK), :]
        total = cum[CHUNK - 1:CHUNK, :]
        k_dec = (k_s[pl.ds(r0, CHUNK), :] * jnp.exp(total - cum)).astype(jnp.bfloat16)
        e_tot = jnp.exp(total)
        for h in range(heads):
            ks = slice(h * dk, (h + 1) * dk)
            vs = slice(h * dv, (h + 1) * dv)
            v_h = v_s[pl.ds(r0, CHUNK), vs]
            upd = lax.dot_general(v_h, k_dec[:, ks], (((0,), (0,)), ((), ())),
                                  preferred_element_type=jnp.float32)
            st = state[h] * e_tot[:, ks] + upd
            state[h] = st
            o_h = lax.dot_general(q_s[pl.ds(r0, CHUNK), ks], st.astype(jnp.bfloat16),
                                  (((1,), (1,)), ((), ())),
                                  preferred_element_type=jnp.float32)
            o_h = _rmsnorm(o_h, gnorm_ref[h:h + 1, :])
            ob_s[pl.ds(r0, CHUNK), vs] = (o_h * g_s[pl.ds(r0, CHUNK), vs]).astype(jnp.bfloat16)
        return carry

    lax.fori_loop(0, tt // CHUNK, gla_body, 0)

    merged = merged_s[...] + gate_s[:, d_conv:] * _dot(ob_s[...], wgo_ref[...])
    o_ref[...] = x + _dot(merged.astype(jnp.bfloat16), wout_ref[...])


def _resident(arr):
    nd = arr.ndim
    return pl.BlockSpec(arr.shape, lambda b, t: (0,) * nd, pipeline_mode=pl.Buffered(1))


def _mixer_call(x, nmix, wglu, wqkvg, wglow, wmerge, bmerge, cdw, cdwb, lng, lnb, wco,
                wgk2, bgk, gnorm, wgo, wout, tri):
    bsz, seq, d = x.shape
    tt = MIX_TT
    conv_w, d_conv = cdw.shape
    dk_total = wgk2.shape[1]
    dv_total = wgo.shape[0]
    halo = _round_up(conv_w - 1, SUBLANES)
    assert seq % tt == 0 and tt % CHUNK == 0 and tt % CONV_ROWS == 0
    assert halo <= tt and d_conv == d

    kern = functools.partial(_mixer_kernel, tt=tt, halo=halo, conv_w=conv_w, d_conv=d_conv,
                             dk_total=dk_total, dv_total=dv_total)
    params = (nmix, wglu, wqkvg, wglow, wmerge, bmerge, cdw, cdwb, lng, lnb, wco, wgk2, bgk,
              gnorm, wgo, wout, tri)
    x_spec = pl.BlockSpec((None, tt, d), lambda b, t: (b, t, 0))
    f32, bf16 = jnp.float32, jnp.bfloat16
    return pl.pallas_call(
        kern,
        grid=(bsz, seq // tt),
        in_specs=[x_spec] + [_resident(p) for p in params],
        out_specs=x_spec,
        out_shape=jax.ShapeDtypeStruct(x.shape, x.dtype),
        scratch_shapes=[
            pltpu.VMEM((halo + tt, d_conv), f32),
            pltpu.VMEM((tt, d_conv), bf16),
            pltpu.VMEM((tt, dk_total), bf16),
            pltpu.VMEM((tt, dk_total), f32),
            pltpu.VMEM((tt, dv_total), bf16),
            pltpu.VMEM((tt, dv_total), f32),
            pltpu.VMEM((tt, dk_total), f32),
            pltpu.VMEM((tt, 2 * d), f32),
            pltpu.VMEM((tt, d), f32),
            pltpu.VMEM((tt, dv_total), bf16),
            pltpu.VMEM((GLA_HEADS, dv_total // GLA_HEADS, dk_total // GLA_HEADS), f32),
        ],
        compiler_params=pltpu.CompilerParams(
            dimension_semantics=("arbitrary", "arbitrary"),
            vmem_limit_bytes=MIX_VMEM_BYTES),
        name="mixer",
    )(x, *params)


def _ffn_kernel(x_ref, nffn_ref, wup_ref, fdw_ref, fdwb_ref, wdown_ref, nfin_ref,
                o_ref, upbuf, act_s, *, tt, halo, conv_w, d_ff, col_blk, final_norm):
    t = pl.program_id(1)

    @pl.when(t == 0)
    def _():
        upbuf[0:halo, :] = jnp.zeros((halo, 2 * d_ff), jnp.float32)

    @pl.when(t > 0)
    def _():
        upbuf[0:halo, :] = upbuf[tt:tt + halo, :]

    x = x_ref[...]
    hb = _rmsnorm(x, nffn_ref[...]).astype(jnp.bfloat16)
    base = halo - (conv_w - 1)

    def conv(cols):
        upbuf[halo:halo + tt, cols] = _dot(hb, wup_ref[:, cols])
        acc = fdwb_ref[:, cols] + fdw_ref[0:1, cols] * upbuf[base:base + tt, cols]
        for j in range(1, conv_w):
            acc = acc + fdw_ref[j:j + 1, cols] * upbuf[base + j:base + j + tt, cols]
        return acc

    for nb in range(d_ff // col_blk):
        u_gate = conv(slice(nb * col_blk, (nb + 1) * col_blk))
        u_val = conv(slice(d_ff + nb * col_blk, d_ff + (nb + 1) * col_blk))
        act_s[:, nb * col_blk:(nb + 1) * col_blk] = (
            u_gate * _sigmoid(u_gate) * u_val).astype(jnp.bfloat16)

    y = x + _dot(act_s[...], wdown_ref[...])
    if final_norm:
        y = _rmsnorm(y, nfin_ref[...])
    o_ref[...] = y


def _ffn_call(x, nffn, wup, fdw, fdwb, wdown, nfin, final_norm):
    bsz, seq, d = x.shape
    tt = FFN_TT
    conv_w = fdw.shape[0]
    d_ff = wdown.shape[0]
    halo = _round_up(conv_w - 1, SUBLANES)
    col_blk = MXU_DIM
    assert seq % tt == 0 and d_ff % col_blk == 0 and halo <= tt

    kern = functools.partial(_ffn_kernel, tt=tt, halo=halo, conv_w=conv_w, d_ff=d_ff,
                             col_blk=col_blk, final_norm=final_norm)
    params = (nffn, wup, fdw, fdwb, wdown, nfin)
    x_spec = pl.BlockSpec((None, tt, d), lambda b, t: (b, t, 0))
    return pl.pallas_call(
        kern,
        grid=(bsz, seq // tt),
        in_specs=[x_spec] + [_resident(p) for p in params],
        out_specs=x_spec,
        out_shape=jax.ShapeDtypeStruct(x.shape, x.dtype),
        scratch_shapes=[
            pltpu.VMEM((halo + tt, 2 * d_ff), jnp.float32),
            pltpu.VMEM((tt, d_ff), jnp.bfloat16),
        ],
        compiler_params=pltpu.CompilerParams(
            dimension_semantics=("arbitrary", "arbitrary"),
            vmem_limit_bytes=FFN_VMEM_BYTES),
        name="ffn",
    )(x, *params)


def _chunk_tri(tt):
    r = jnp.arange(tt)
    same_chunk = (r[:, None] // CHUNK) == (r[None, :] // CHUNK)
    return (same_chunk & (r[None, :] <= r[:, None])).astype(jnp.bfloat16)


def kernel(x, norm_mix, w_in, b_merge, conv_dw, conv_dw_b, conv_ln_g, conv_ln_b, w_conv_out,
           w_gk2, b_gk, gla_norm, w_gla_out, w_out, norm_ffn, w_up, ffn_dw, ffn_dw_b, w_down,
           norm_final):
    depth = w_in.shape[0]
    d = x.shape[-1]
    d_conv = conv_dw.shape[-1]
    rank, dk_total = w_gk2.shape[1], w_gk2.shape[2]
    dv_total = w_gla_out.shape[1]
    bf16 = jnp.bfloat16
    row = lambda v: v.reshape(1, -1)

    o_glu = 2 * d_conv
    o_qkvg = o_glu + 2 * dk_total + 2 * dv_total
    o_low = o_qkvg + rank
    rank_pad = _round_up(rank, LANES)
    tri = _chunk_tri(MIX_TT)

    for l in range(depth):
        wl = w_in[l]
        wglow = jnp.pad(wl[:, o_qkvg:o_low], ((0, 0), (0, rank_pad - rank))).astype(bf16)
        wgk2 = jnp.pad(w_gk2[l], ((0, rank_pad - rank), (0, 0))).astype(bf16)
        x = _mixer_call(
            x, row(norm_mix[l]), wl[:, :o_glu].astype(bf16), wl[:, o_glu:o_qkvg].astype(bf16),
            wglow, wl[:, o_low:].astype(bf16), row(b_merge[l]), conv_dw[l], row(conv_dw_b[l]),
            row(conv_ln_g[l]), row(conv_ln_b[l]), w_conv_out[l].astype(bf16), wgk2,
            row(b_gk[l]), gla_norm[l], w_gla_out[l].astype(bf16), w_out[l].astype(bf16), tri)
        x = _ffn_call(
            x, row(norm_ffn[l]), w_up[l].astype(bf16), ffn_dw[l], row(ffn_dw_b[l]),
            w_down[l].astype(bf16), row(norm_final), final_norm=(l == depth - 1))
    return x
```

```python
import functools

import jax
import jax.numpy as jnp
from jax import lax
from jax.experimental import pallas as pl
from jax.experimental.pallas import tpu as pltpu

CHUNK = 64
GLA_HEADS = 4
GATE_TEMP = 16.0
EPS = 1e-6

SUBLANES = 8
LANES = 128
MXU_DIM = 256

MIX_TT = 256
FFN_TT = 256
CONV_ROWS = 32
GLU_COLS = MXU_DIM
MIX_VMEM_BYTES = 52 * 1024 * 1024
FFN_VMEM_BYTES = 52 * 1024 * 1024


def _round_up(n, m):
    return (n + m - 1) // m * m


def _dot(a, b):
    return jnp.dot(a, b, preferred_element_type=jnp.float32)


def _sigmoid(x):
    return 1.0 / (1.0 + jnp.exp(-x))


def _rmsnorm(x, g):
    return x * lax.rsqrt(jnp.mean(x * x, axis=-1, keepdims=True) + EPS) * g


def _mixer_kernel(x_ref, nmix_ref, wglu_ref, wqkvg_ref, wglow_ref, wmerge_ref, bmerge_ref,
                  cdw_ref, cdwb_ref, lng_ref, lnb_ref, wco_ref, wgk2_ref, bgk_ref,
                  gnorm_ref, wgo_ref, wout_ref, tri_ref,
                  o_ref,
                  abuf, cbuf, ca_s, q_s, k_s, v_s, g_s, cum_s, gate_s, merged_s, ob_s, state,
                  *, tt, halo, conv_w, d_conv, dk_total, dv_total):
    heads = GLA_HEADS
    n_slab = d_conv // LANES
    dk = dk_total // heads
    dv = dv_total // heads
    t = pl.program_id(1)

    @pl.when(t == 0)
    def _():
        abuf[:, 0:halo, :] = jnp.zeros((n_slab, halo, LANES), jnp.float32)
        state[...] = jnp.zeros_like(state)

    @pl.when(t > 0)
    def _():
        abuf[:, 0:halo, :] = abuf[:, tt:tt + halo, :]

    x = x_ref[...]
    hb = _rmsnorm(x, nmix_ref[...]).astype(jnp.bfloat16)

    base = halo - (conv_w - 1)
    glu_cols = GLU_COLS
    n_blk = d_conv // glu_cols

    def glu_block(nb):
        lo = nb * glu_cols
        val = _dot(hb, wglu_ref[:, lo:lo + glu_cols])
        gate = _dot(hb, wglu_ref[:, d_conv + lo:d_conv + lo + glu_cols])
        a = val * _sigmoid(gate)
        for c in range(glu_cols // LANES):
            abuf[lo // LANES + c, halo:halo + tt, :] = a[:, c * LANES:(c + 1) * LANES]

    def conv_block(nb):
        for c in range(nb * glu_cols // LANES, (nb + 1) * glu_cols // LANES):
            lanes = slice(c * LANES, (c + 1) * LANES)
            for r0 in range(0, tt, CONV_ROWS):
                acc = cdwb_ref[:, lanes] + cdw_ref[0:1, lanes] * abuf[
                    c, pl.ds(r0 + base, CONV_ROWS, stride=1), :]
                for j in range(1, conv_w):
                    acc = acc + cdw_ref[j:j + 1, lanes] * abuf[
                        c, pl.ds(r0 + base + j, CONV_ROWS, stride=1), :]
                cbuf[r0:r0 + CONV_ROWS, lanes] = acc

    def proj_qk():
        qk = _dot(hb, wqkvg_ref[:, :2 * dk_total])
        q_s[...] = (qk[:, :dk_total] * (dk ** -0.5)).astype(jnp.bfloat16)
        k_s[...] = qk[:, dk_total:]

    def proj_v():
        v_s[...] = _dot(hb, wqkvg_ref[:, 2 * dk_total:2 * dk_total + dv_total]).astype(jnp.bfloat16)

    def proj_g():
        g_out = _dot(hb, wqkvg_ref[:, 2 * dk_total + dv_total:])
        g_s[...] = g_out * _sigmoid(g_out)

    def proj_decay():
        g_low = _dot(hb, wglow_ref[...])
        gk = _dot(g_low.astype(jnp.bfloat16), wgk2_ref[...]) + bgk_ref[...]
        log_a = (jnp.minimum(gk, 0.0) - jnp.log1p(jnp.exp(-jnp.abs(gk)))) * (1.0 / GATE_TEMP)
        la_hi = log_a.astype(jnp.bfloat16)
        la_lo = (log_a - la_hi.astype(jnp.float32)).astype(jnp.bfloat16)
        tri = tri_ref[...]
        cum_s[...] = _dot(tri, la_hi) + _dot(tri, la_lo)

    def proj_merge():
        gate_s[...] = _sigmoid(_dot(hb, wmerge_ref[...]) + bmerge_ref[...])

    side_work = [proj_qk, proj_v, proj_g, proj_decay, proj_merge]
    glu_block(0)
    for nb in range(n_blk):
        if nb + 1 < n_blk:
            glu_block(nb + 1)
        conv_block(nb)
        if side_work:
            side_work.pop(0)()
    for fn in side_work:
        fn()

    for r0 in range(0, tt, CONV_ROWS):
        acc = cbuf[r0:r0 + CONV_ROWS, :]
        mu = jnp.mean(acc, axis=-1, keepdims=True)
        xc = acc - mu
        y = xc * lax.rsqrt(jnp.mean(xc * xc, axis=-1, keepdims=True) + EPS)
        y = y * lng_ref[...] + lnb_ref[...]
        ca_s[r0:r0 + CONV_ROWS, :] = (y * _sigmoid(y)).astype(jnp.bfloat16)
    merged_s[...] = gate_s[:, :d_conv] * _dot(ca_s[...], wco_ref[...])

    chunks = list(range(0, tt, CHUNK))
    e_tots, upds = [], {}
    for ci, r0 in enumerate(chunks):
        cum = cum_s[r0:r0 + CHUNK, :]
        total = cum[CHUNK - 1:CHUNK, :]
        k_dec = (k_s[r0:r0 + CHUNK, :] * jnp.exp(total - cum)).astype(jnp.bfloat16)
        e_tots.append(jnp.exp(total))
        for h in range(heads):
            v_h = v_s[r0:r0 + CHUNK, h * dv:(h + 1) * dv]
            upds[ci, h] = lax.dot_general(v_h, k_dec[:, h * dk:(h + 1) * dk],
                                          (((0,), (0,)), ((), ())),
                                          preferred_element_type=jnp.float32)
    sts = {}
    for h in range(heads):
        st = state[h]
        for ci in range(len(chunks)):
            st = st * e_tots[ci][:, h * dk:(h + 1) * dk] + upds[ci, h]
            sts[ci, h] = st.astype(jnp.bfloat16)
        state[h] = st
    for ci, r0 in enumerate(chunks):
        for h in range(heads):
            vs = slice(h * dv, (h + 1) * dv)
            o_h = lax.dot_general(q_s[r0:r0 + CHUNK, h * dk:(h + 1) * dk], sts[ci, h],
                                  (((1,), (1,)), ((), ())),
                                  preferred_element_type=jnp.float32)
            o_h = _rmsnorm(o_h, gnorm_ref[h:h + 1, :])
            ob_s[r0:r0 + CHUNK, vs] = (o_h * g_s[r0:r0 + CHUNK, vs]).astype(jnp.bfloat16)


    merged = merged_s[...] + gate_s[:, d_conv:] * _dot(ob_s[...], wgo_ref[...])
    o_ref[...] = x + _dot(merged.astype(jnp.bfloat16), wout_ref[...])


def _resident(arr):
    nd = arr.ndim
    return pl.BlockSpec(arr.shape, lambda b, t: (0,) * nd, pipeline_mode=pl.Buffered(1))


def _mixer_call(x, nmix, wglu, wqkvg, wglow, wmerge, bmerge, cdw, cdwb, lng, lnb, wco,
                wgk2, bgk, gnorm, wgo, wout, tri):
    bsz, seq, d = x.shape
    tt = MIX_TT
    conv_w, d_conv = cdw.shape
    dk_total = wgk2.shape[1]
    dv_total = wgo.shape[0]
    halo = _round_up(conv_w - 1, SUBLANES)
    assert seq % tt == 0 and tt % CHUNK == 0 and tt % CONV_ROWS == 0
    assert halo <= tt and d_conv == d and d_conv % LANES == 0

    kern = functools.partial(_mixer_kernel, tt=tt, halo=halo, conv_w=conv_w, d_conv=d_conv,
                             dk_total=dk_total, dv_total=dv_total)
    params = (nmix, wglu, wqkvg, wglow, wmerge, bmerge, cdw, cdwb, lng, lnb, wco, wgk2, bgk,
              gnorm, wgo, wout, tri)
    x_spec = pl.BlockSpec((None, tt, d), lambda b, t: (b, t, 0))
    f32, bf16 = jnp.float32, jnp.bfloat16
    return pl.pallas_call(
        kern,
        grid=(bsz, seq // tt),
        in_specs=[x_spec] + [_resident(p) for p in params],
        out_specs=x_spec,
        out_shape=jax.ShapeDtypeStruct(x.shape, x.dtype),
        scratch_shapes=[
            pltpu.VMEM((d_conv // LANES, halo + tt, LANES), f32),
            pltpu.VMEM((tt, d_conv), f32),
            pltpu.VMEM((tt, d_conv), bf16),
            pltpu.VMEM((tt, dk_total), bf16),
            pltpu.VMEM((tt, dk_total), f32),
            pltpu.VMEM((tt, dv_total), bf16),
            pltpu.VMEM((tt, dv_total), f32),
            pltpu.VMEM((tt, dk_total), f32),
            pltpu.VMEM((tt, 2 * d), f32),
            pltpu.VMEM((tt, d), f32),
            pltpu.VMEM((tt, dv_total), bf16),
            pltpu.VMEM((GLA_HEADS, dv_total // GLA_HEADS, dk_total // GLA_HEADS), f32),
        ],
        compiler_params=pltpu.CompilerParams(
            dimension_semantics=("arbitrary", "arbitrary"),
            vmem_limit_bytes=MIX_VMEM_BYTES),
        name="mixer",
    )(x, *params)


def _ffn_kernel(x_ref, nffn_ref, wup_ref, fdw_ref, fdwb_ref, wdown_ref, nfin_ref,
                o_ref, upbuf, act_s, *, tt, halo, conv_w, d_ff, col_blk, final_norm):
    t = pl.program_id(1)

    @pl.when(t == 0)
    def _():
        upbuf[:, 0:halo, :] = jnp.zeros((2 * d_ff // LANES, halo, LANES), jnp.float32)

    @pl.when(t > 0)
    def _():
        upbuf[:, 0:halo, :] = upbuf[:, tt:tt + halo, :]

    x = x_ref[...]
    hb = _rmsnorm(x, nffn_ref[...]).astype(jnp.bfloat16)
    base = halo - (conv_w - 1)

    def conv(col0):
        up = _dot(hb, wup_ref[:, col0:col0 + col_blk])
        slabs = []
        for c in range(col_blk // LANES):
            s = col0 // LANES + c
            lanes = slice(col0 + c * LANES, col0 + (c + 1) * LANES)
            upbuf[s, halo:halo + tt, :] = up[:, c * LANES:(c + 1) * LANES]
            acc = fdwb_ref[:, lanes] + fdw_ref[0:1, lanes] * upbuf[s, pl.ds(base, tt, stride=1), :]
            for j in range(1, conv_w):
                acc = acc + fdw_ref[j:j + 1, lanes] * upbuf[s, pl.ds(base + j, tt, stride=1), :]
            slabs.append(acc)
        return jnp.concatenate(slabs, axis=-1)

    for nb in range(d_ff // col_blk):
        u_gate = conv(nb * col_blk)
        u_val = conv(d_ff + nb * col_blk)
        act_s[:, nb * col_blk:(nb + 1) * col_blk] = (
            u_gate * _sigmoid(u_gate) * u_val).astype(jnp.bfloat16)

    y = x + _dot(act_s[...], wdown_ref[...])
    if final_norm:
        y = _rmsnorm(y, nfin_ref[...])
    o_ref[...] = y


def _ffn_call(x, nffn, wup, fdw, fdwb, wdown, nfin, final_norm):
    bsz, seq, d = x.shape
    tt = FFN_TT
    conv_w = fdw.shape[0]
    d_ff = wdown.shape[0]
    halo = _round_up(conv_w - 1, SUBLANES)
    col_blk = MXU_DIM
    assert seq % tt == 0 and d_ff % col_blk == 0 and halo <= tt

    kern = functools.partial(_ffn_kernel, tt=tt, halo=halo, conv_w=conv_w, d_ff=d_ff,
                             col_blk=col_blk, final_norm=final_norm)
    params = (nffn, wup, fdw, fdwb, wdown, nfin)
    x_spec = pl.BlockSpec((None, tt, d), lambda b, t: (b, t, 0))
    return pl.pallas_call(
        kern,
        grid=(bsz, seq // tt),
        in_specs=[x_spec] + [_resident(p) for p in params],
        out_specs=x_spec,
        out_shape=jax.ShapeDtypeStruct(x.shape, x.dtype),
        scratch_shapes=[
            pltpu.VMEM((2 * d_ff // LANES, halo + tt, LANES), jnp.float32),
            pltpu.VMEM((tt, d_ff), jnp.bfloat16),
        ],
        compiler_params=pltpu.CompilerParams(
            dimension_semantics=("arbitrary", "arbitrary"),
            vmem_limit_bytes=FFN_VMEM_BYTES),
        name="ffn",
    )(x, *params)


def _chunk_tri(tt):
    r = jnp.arange(tt)
    same_chunk = (r[:, None] // CHUNK) == (r[None, :] // CHUNK)
    return (same_chunk & (r[None, :] <= r[:, None])).astype(jnp.bfloat16)


def kernel(x, norm_mix, w_in, b_merge, conv_dw, conv_dw_b, conv_ln_g, conv_ln_b, w_conv_out,
           w_gk2, b_gk, gla_norm, w_gla_out, w_out, norm_ffn, w_up, ffn_dw, ffn_dw_b, w_down,
           norm_final):
    depth = w_in.shape[0]
    d = x.shape[-1]
    d_conv = conv_dw.shape[-1]
    rank, dk_total = w_gk2.shape[1], w_gk2.shape[2]
    dv_total = w_gla_out.shape[1]
    bf16 = jnp.bfloat16
    row = lambda v: v.reshape(1, -1)

    o_glu = 2 * d_conv
    o_qkvg = o_glu + 2 * dk_total + 2 * dv_total
    o_low = o_qkvg + rank
    rank_pad = _round_up(rank, LANES)
    tri = _chunk_tri(MIX_TT)

    for l in range(depth):
        wl = w_in[l]
        wglow = jnp.pad(wl[:, o_qkvg:o_low], ((0, 0), (0, rank_pad - rank))).astype(bf16)
        wgk2 = jnp.pad(w_gk2[l], ((0, rank_pad - rank), (0, 0))).astype(bf16)
        x = _mixer_call(
            x, row(norm_mix[l]), wl[:, :o_glu].astype(bf16), wl[:, o_glu:o_qkvg].astype(bf16),
            wglow, wl[:, o_low:].astype(bf16), row(b_merge[l]), conv_dw[l], row(conv_dw_b[l]),
            row(conv_ln_g[l]), row(conv_ln_b[l]), w_conv_out[l].astype(bf16), wgk2,
            row(b_gk[l]), gla_norm[l], w_gla_out[l].astype(bf16), w_out[l].astype(bf16), tri)
        x = _ffn_call(
            x, row(norm_ffn[l]), w_up[l].astype(bf16), ffn_dw[l], row(ffn_dw_b[l]),
            w_down[l].astype(bf16), row(norm_final), final_norm=(l == depth - 1))
    return x
```

```python
import functools
import math

import jax
import jax.numpy as jnp
from jax import lax
from jax.experimental import pallas as pl
from jax.experimental.pallas import tpu as pltpu

CHUNK = 64
GLA_HEADS = 4
GATE_TEMP = 16.0
EPS = 1e-6

SUBLANES = 8
LANES = 128
MXU_DIM = 256

MIX_TT = 512
MIX_SUB = 512
MIX_SUB2 = 256
FFN_TT = 512
CONV_ROWS = 32
GLU_COLS = MXU_DIM
MIX_VMEM_BYTES = 56 * 1024 * 1024
FFN_VMEM_BYTES = 56 * 1024 * 1024

_NEG_LOG2E = -1.0 / math.log(2.0)


def _round_up(n, m):
    return (n + m - 1) // m * m


def _dot(a, b):
    return jnp.dot(a, b, preferred_element_type=jnp.float32)


def _sigmoid(x):
    return 1.0 / (1.0 + jnp.exp2(x * _NEG_LOG2E))


def _rmsnorm(x, g):
    return x * lax.rsqrt(jnp.mean(x * x, axis=-1, keepdims=True) + EPS) * g


def _resident(arr):
    nd = arr.ndim
    return pl.BlockSpec(arr.shape, lambda *_: (0,) * nd, pipeline_mode=pl.Buffered(1))


def _mixer_kernel(x_ref, nmix_ref, wglu_ref, wqkvg_ref, wglow_ref, wmerge_ref, bmerge_ref,
                  cdw_ref, cdwb_ref, lng_ref, lnb_ref, wco_ref, wgk2_ref, bgk_ref,
                  gnorm_ref, wgo_ref, wout_ref, tri_ref,
                  o_ref,
                  abuf, cbuf, ca_s, q_s, k_s, v_s, g_s, cum_s, gate_s, merged_s, ob_s, state,
                  *, tt, sub, sub2, halo, conv_w, d_conv, dk_total, dv_total):
    heads = GLA_HEADS
    n_slab = d_conv // LANES
    dk = dk_total // heads
    dv = dv_total // heads
    t = pl.program_id(1)

    @pl.when(t == 0)
    def _():
        abuf[:, 0:halo, :] = jnp.zeros((n_slab, halo, LANES), jnp.float32)
        state[...] = jnp.zeros_like(state)

    @pl.when(t > 0)
    def _():
        abuf[:, 0:halo, :] = abuf[:, tt:tt + halo, :]

    base = halo - (conv_w - 1)
    glu_cols = GLU_COLS
    n_blk = d_conv // glu_cols

    def stage1(row0):
        rows = slice(row0, row0 + sub)
        vals = {}

        def norm():
            vals["hb"] = _rmsnorm(x_ref[rows, :], nmix_ref[...]).astype(jnp.bfloat16)

        def glu_block(nb):
            lo = nb * glu_cols
            val = _dot(vals["hb"], wglu_ref[:, lo:lo + glu_cols])
            gate = _dot(vals["hb"], wglu_ref[:, d_conv + lo:d_conv + lo + glu_cols])
            a = val * _sigmoid(gate)
            for c in range(glu_cols // LANES):
                abuf[lo // LANES + c, halo + row0:halo + row0 + sub, :] = a[:, c * LANES:(c + 1) * LANES]

        def conv_block(nb):
            for c in range(nb * glu_cols // LANES, (nb + 1) * glu_cols // LANES):
                lanes = slice(c * LANES, (c + 1) * LANES)
                for r0 in range(row0, row0 + sub, CONV_ROWS):
                    acc = cdwb_ref[:, lanes] + cdw_ref[0:1, lanes] * abuf[
                        c, pl.ds(r0 + base, CONV_ROWS, stride=1), :]
                    for j in range(1, conv_w):
                        acc = acc + cdw_ref[j:j + 1, lanes] * abuf[
                            c, pl.ds(r0 + base + j, CONV_ROWS, stride=1), :]
                    cbuf[r0:r0 + CONV_ROWS, lanes] = acc

        def proj_qk():
            qk = _dot(vals["hb"], wqkvg_ref[:, :2 * dk_total])
            q_s[rows, :] = (qk[:, :dk_total] * (dk ** -0.5)).astype(jnp.bfloat16)
            k_s[rows, :] = qk[:, dk_total:]

        def proj_v():
            v_s[rows, :] = _dot(
                vals["hb"], wqkvg_ref[:, 2 * dk_total:2 * dk_total + dv_total]).astype(jnp.bfloat16)

        def proj_g():
            g_s[rows, :] = _dot(vals["hb"], wqkvg_ref[:, 2 * dk_total + dv_total:])

        def decay_low():
            vals["g_low"] = _dot(vals["hb"], wglow_ref[...])

        def decay_gate():
            gk = _dot(vals["g_low"].astype(jnp.bfloat16), wgk2_ref[...]) + bgk_ref[...]
            log_a = (jnp.minimum(gk, 0.0) - jnp.log1p(jnp.exp(-jnp.abs(gk)))) * (1.0 / GATE_TEMP)
            la_hi = log_a.astype(jnp.bfloat16)
            vals["hi"] = la_hi
            vals["lo"] = (log_a - la_hi.astype(jnp.float32)).astype(jnp.bfloat16)

        def decay_cumsum():
            tri = tri_ref[...]
            cum_s[rows, :] = _dot(tri, vals["hi"]) + _dot(tri, vals["lo"])

        def proj_merge():
            gate_s[rows, :] = _dot(vals["hb"], wmerge_ref[...])

        side = [decay_low, proj_qk, decay_gate, proj_v, decay_cumsum, proj_g, proj_merge]
        pieces = [[norm, functools.partial(glu_block, 0)]]
        for nb in range(n_blk):
            piece = [functools.partial(glu_block, nb + 1)] if nb + 1 < n_blk else []
            piece.append(functools.partial(conv_block, nb))
            piece.extend(side[2 * nb:2 * nb + 2])
            pieces.append(piece)
        pieces[-1].extend(side[2 * n_blk:])
        return pieces

    def stage2(row0):
        rows = slice(row0, row0 + sub2)
        chunks = list(range(row0, row0 + sub2, CHUNK))
        vals = {}

        def layernorm():
            for r0 in range(row0, row0 + sub2, CONV_ROWS):
                acc = cbuf[r0:r0 + CONV_ROWS, :]
                mu = jnp.mean(acc, axis=-1, keepdims=True)
                xc = acc - mu
                y = xc * lax.rsqrt(jnp.mean(xc * xc, axis=-1, keepdims=True) + EPS)
                y = y * lng_ref[...] + lnb_ref[...]
                ca_s[r0:r0 + CONV_ROWS, :] = (y * _sigmoid(y)).astype(jnp.bfloat16)

        def conv_out():
            gate_a = _sigmoid(gate_s[rows, :d_conv] + bmerge_ref[:, :d_conv])
            merged_s[rows, :] = gate_a * _dot(ca_s[rows, :], wco_ref[...])

        def gla_update():
            e_tots, upds = [], {}
            for ci, r0 in enumerate(chunks):
                cum = cum_s[r0:r0 + CHUNK, :]
                total = cum[CHUNK - 1:CHUNK, :]
                k_dec = (k_s[r0:r0 + CHUNK, :] * jnp.exp(total - cum)).astype(jnp.bfloat16)
                e_tots.append(jnp.exp(total))
                for h in range(heads):
                    v_h = v_s[r0:r0 + CHUNK, h * dv:(h + 1) * dv]
                    upds[ci, h] = lax.dot_general(v_h, k_dec[:, h * dk:(h + 1) * dk],
                                                  (((0,), (0,)), ((), ())),
                                                  preferred_element_type=jnp.float32)
            for h in range(heads):
                st = state[h]
                for ci in range(len(chunks)):
                    st = st * e_tots[ci][:, h * dk:(h + 1) * dk] + upds[ci, h]
                    vals[ci, h] = st.astype(jnp.bfloat16)
                state[h] = st

        def gla_readout():
            for ci, r0 in enumerate(chunks):
                for h in range(heads):
                    vs = slice(h * dv, (h + 1) * dv)
                    o_h = lax.dot_general(q_s[r0:r0 + CHUNK, h * dk:(h + 1) * dk], vals[ci, h],
                                          (((1,), (1,)), ((), ())),
                                          preferred_element_type=jnp.float32)
                    o_h = _rmsnorm(o_h, gnorm_ref[h:h + 1, :])
                    g_out = g_s[r0:r0 + CHUNK, vs]
                    ob_s[r0:r0 + CHUNK, vs] = (o_h * (g_out * _sigmoid(g_out))).astype(jnp.bfloat16)

        def output():
            gate_b = _sigmoid(gate_s[rows, d_conv:] + bmerge_ref[:, d_conv:])
            merged = merged_s[rows, :] + gate_b * _dot(ob_s[rows, :], wgo_ref[...])
            o_ref[rows, :] = x_ref[rows, :] + _dot(merged.astype(jnp.bfloat16), wout_ref[...])

        return [[layernorm], [conv_out], [gla_update], [gla_readout], [output]]

    for row0 in range(0, tt, sub):
        for piece in stage1(row0):
            for fn in piece:
                fn()
    stage2_tiles = [stage2(row0) for row0 in range(0, tt, sub2)]
    for step in zip(*stage2_tiles):
        for piece in step:
            for fn in piece:
                fn()


def _mixer_call(x, nmix, wglu, wqkvg, wglow, wmerge, bmerge, cdw, cdwb, lng, lnb, wco,
                wgk2, bgk, gnorm, wgo, wout, tri):
    bsz, seq, d = x.shape
    tt, sub, sub2 = MIX_TT, MIX_SUB, MIX_SUB2
    conv_w, d_conv = cdw.shape
    dk_total = wgk2.shape[1]
    dv_total = wgo.shape[0]
    halo = _round_up(conv_w - 1, SUBLANES)
    assert seq % tt == 0 and tt % sub == 0 and sub % CHUNK == 0 and sub % CONV_ROWS == 0
    assert tt % sub2 == 0 and sub2 % CHUNK == 0 and sub2 % CONV_ROWS == 0
    assert halo <= tt and d_conv == d and d_conv % GLU_COLS == 0 and tri.shape == (sub, sub)

    kern = functools.partial(_mixer_kernel, tt=tt, sub=sub, sub2=sub2, halo=halo, conv_w=conv_w,
                             d_conv=d_conv, dk_total=dk_total, dv_total=dv_total)
    params = (nmix, wglu, wqkvg, wglow, wmerge, bmerge, cdw, cdwb, lng, lnb, wco, wgk2, bgk,
              gnorm, wgo, wout, tri)
    x_spec = pl.BlockSpec((None, tt, d), lambda b, t: (b, t, 0))
    f32, bf16 = jnp.float32, jnp.bfloat16
    return pl.pallas_call(
        kern,
        grid=(bsz, seq // tt),
        in_specs=[x_spec] + [_resident(p) for p in params],
        out_specs=x_spec,
        out_shape=jax.ShapeDtypeStruct(x.shape, x.dtype),
        scratch_shapes=[
            pltpu.VMEM((d_conv // LANES, halo + tt, LANES), f32),
            pltpu.VMEM((tt, d_conv), f32),
            pltpu.VMEM((tt, d_conv), bf16),
            pltpu.VMEM((tt, dk_total), bf16),
            pltpu.VMEM((tt, dk_total), f32),
            pltpu.VMEM((tt, dv_total), bf16),
            pltpu.VMEM((tt, dv_total), f32),
            pltpu.VMEM((tt, dk_total), f32),
            pltpu.VMEM((tt, 2 * d), f32),
            pltpu.VMEM((tt, d), f32),
            pltpu.VMEM((tt, dv_total), bf16),
            pltpu.VMEM((GLA_HEADS, dv_total // GLA_HEADS, dk_total // GLA_HEADS), f32),
        ],
        compiler_params=pltpu.CompilerParams(
            dimension_semantics=("arbitrary", "arbitrary"),
            vmem_limit_bytes=MIX_VMEM_BYTES),
        name="mixer",
    )(x, *params)


def _ffn_kernel(x_ref, nffn_ref, wup_ref, fdw_ref, fdwb_ref, wdown_ref, nfin_ref,
                o_ref, upbuf, act_s, *, tt, halo, conv_w, d_ff, col_blk, final_norm):
    t = pl.program_id(1)

    @pl.when(t == 0)
    def _():
        upbuf[:, 0:halo, :] = jnp.zeros((2 * d_ff // LANES, halo, LANES), jnp.float32)

    @pl.when(t > 0)
    def _():
        upbuf[:, 0:halo, :] = upbuf[:, tt:tt + halo, :]

    x = x_ref[...]
    hb = _rmsnorm(x, nffn_ref[...]).astype(jnp.bfloat16)
    base = halo - (conv_w - 1)

    def conv(col0):
        up = _dot(hb, wup_ref[:, col0:col0 + col_blk])
        slabs = []
        for c in range(col_blk // LANES):
            s = col0 // LANES + c
            lanes = slice(col0 + c * LANES, col0 + (c + 1) * LANES)
            upbuf[s, halo:halo + tt, :] = up[:, c * LANES:(c + 1) * LANES]
            acc = fdwb_ref[:, lanes] + fdw_ref[0:1, lanes] * upbuf[s, pl.ds(base, tt, stride=1), :]
            for j in range(1, conv_w):
                acc = acc + fdw_ref[j:j + 1, lanes] * upbuf[s, pl.ds(base + j, tt, stride=1), :]
            slabs.append(acc)
        return jnp.concatenate(slabs, axis=-1)

    for nb in range(d_ff // col_blk):
        u_gate = conv(nb * col_blk)
        u_val = conv(d_ff + nb * col_blk)
        act_s[:, nb * col_blk:(nb + 1) * col_blk] = (
            u_gate * _sigmoid(u_gate) * u_val).astype(jnp.bfloat16)

    y = x + _dot(act_s[...], wdown_ref[...])
    if final_norm:
        y = _rmsnorm(y, nfin_ref[...])
    o_ref[...] = y


def _ffn_call(x, nffn, wup, fdw, fdwb, wdown, nfin, final_norm):
    bsz, seq, d = x.shape
    tt = FFN_TT
    conv_w = fdw.shape[0]
    d_ff = wdown.shape[0]
    halo = _round_up(conv_w - 1, SUBLANES)
    col_blk = MXU_DIM
    assert seq % tt == 0 and d_ff % col_blk == 0 and halo <= tt

    kern = functools.partial(_ffn_kernel, tt=tt, halo=halo, conv_w=conv_w, d_ff=d_ff,
                             col_blk=col_blk, final_norm=final_norm)
    params = (nffn, wup, fdw, fdwb, wdown, nfin)
    x_spec = pl.BlockSpec((None, tt, d), lambda b, t: (b, t, 0))
    return pl.pallas_call(
        kern,
        grid=(bsz, seq // tt),
        in_specs=[x_spec] + [_resident(p) for p in params],
        out_specs=x_spec,
        out_shape=jax.ShapeDtypeStruct(x.shape, x.dtype),
        scratch_shapes=[
            pltpu.VMEM((2 * d_ff // LANES, halo + tt, LANES), jnp.float32),
            pltpu.VMEM((tt, d_ff), jnp.bfloat16),
        ],
        compiler_params=pltpu.CompilerParams(
            dimension_semantics=("arbitrary", "arbitrary"),
            vmem_limit_bytes=FFN_VMEM_BYTES),
        name="ffn",
    )(x, *params)


def _chunk_tri(n):
    r = jnp.arange(n)
    same_chunk = (r[:, None] // CHUNK) == (r[None, :] // CHUNK)
    return (same_chunk & (r[None, :] <= r[:, None])).astype(jnp.bfloat16)


def kernel(x, norm_mix, w_in, b_merge, conv_dw, conv_dw_b, conv_ln_g, conv_ln_b, w_conv_out,
           w_gk2, b_gk, gla_norm, w_gla_out, w_out, norm_ffn, w_up, ffn_dw, ffn_dw_b, w_down,
           norm_final):
    depth = w_in.shape[0]
    d = x.shape[-1]
    d_conv = conv_dw.shape[-1]
    rank, dk_total = w_gk2.shape[1], w_gk2.shape[2]
    dv_total = w_gla_out.shape[1]
    bf16 = jnp.bfloat16
    row = lambda v: v.reshape(1, -1)

    o_glu = 2 * d_conv
    o_qkvg = o_glu + 2 * dk_total + 2 * dv_total
    o_low = o_qkvg + rank
    rank_pad = _round_up(rank, LANES)
    tri = _chunk_tri(MIX_SUB)

    for l in range(depth):
        wl = w_in[l]
        wglow = jnp.pad(wl[:, o_qkvg:o_low], ((0, 0), (0, rank_pad - rank))).astype(bf16)
        wgk2 = jnp.pad(w_gk2[l], ((0, rank_pad - rank), (0, 0))).astype(bf16)
        x = _mixer_call(
            x, row(norm_mix[l]), wl[:, :o_glu].astype(bf16), wl[:, o_glu:o_qkvg].astype(bf16),
            wglow, wl[:, o_low:].astype(bf16), row(b_merge[l]), conv_dw[l], row(conv_dw_b[l]),
            row(conv_ln_g[l]), row(conv_ln_b[l]), w_conv_out[l].astype(bf16), wgk2,
            row(b_gk[l]), gla_norm[l], w_gla_out[l].astype(bf16), w_out[l].astype(bf16), tri)
        x = _ffn_call(
            x, row(norm_ffn[l]), w_up[l].astype(bf16), ffn_dw[l], row(ffn_dw_b[l]),
            w_down[l].astype(bf16), row(norm_final), final_norm=(l == depth - 1))
    return x
```

```python
import functools
import math

import jax
import jax.numpy as jnp
from jax import lax
from jax.experimental import pallas as pl
from jax.experimental.pallas import tpu as pltpu

CHUNK = 64
GLA_HEADS = 4
GATE_TEMP = 16.0
EPS = 1e-6

SUBLANES = 8
LANES = 128
MXU_DIM = 256

MIX_TT = 512
MIX_SUB = 512
MIX_SUB2 = 256
FFN_TT = 512
CONV_ROWS = 32
GLU_COLS = MXU_DIM
MIX_VMEM_BYTES = 56 * 1024 * 1024
FFN_VMEM_BYTES = 56 * 1024 * 1024

_NEG_LOG2E = -1.0 / math.log(2.0)


def _round_up(n, m):
    return (n + m - 1) // m * m


def _dot(a, b):
    return jnp.dot(a, b, preferred_element_type=jnp.float32)


def _sigmoid(x):
    return 1.0 / (1.0 + jnp.exp2(x * _NEG_LOG2E))


def _rmsnorm(x, g):
    return x * lax.rsqrt(jnp.mean(x * x, axis=-1, keepdims=True) + EPS) * g


def _resident(arr):
    nd = arr.ndim
    return pl.BlockSpec(arr.shape, lambda *_: (0,) * nd, pipeline_mode=pl.Buffered(1))


def _mixer_kernel(x_ref, nmix_ref, wglu_ref, wqkvg_ref, wglow_ref, wmerge_ref, bmerge_ref,
                  cdw_ref, cdwb_ref, lng_ref, lnb_ref, wco_ref, wgk2_ref, bgk_ref,
                  gnorm_ref, wgo_ref, wout_ref, tri_ref,
                  o_ref,
                  abuf, cbuf, ca_s, q_s, k_s, v_s, g_s, cum_s, gate_s, merged_s, ob_s, state,
                  *, tt, sub, sub2, halo, conv_w, d_conv, dk_total, dv_total):
    heads = GLA_HEADS
    n_slab = d_conv // LANES
    dk = dk_total // heads
    dv = dv_total // heads
    t = pl.program_id(1)

    @pl.when(t == 0)
    def _():
        abuf[:, 0:halo, :] = jnp.zeros((n_slab, halo, LANES), jnp.float32)
        state[...] = jnp.zeros_like(state)

    @pl.when(t > 0)
    def _():
        abuf[:, 0:halo, :] = abuf[:, tt:tt + halo, :]

    base = halo - (conv_w - 1)
    glu_cols = GLU_COLS
    n_blk = d_conv // glu_cols

    def stage1(row0):
        rows = slice(row0, row0 + sub)
        vals = {}

        def norm():
            vals["hb"] = _rmsnorm(x_ref[rows, :], nmix_ref[...]).astype(jnp.bfloat16)

        def glu_block(nb):
            lo = nb * glu_cols
            val = _dot(vals["hb"], wglu_ref[:, lo:lo + glu_cols])
            gate = _dot(vals["hb"], wglu_ref[:, d_conv + lo:d_conv + lo + glu_cols])
            a = val * _sigmoid(gate)
            for c in range(glu_cols // LANES):
                abuf[lo // LANES + c, halo + row0:halo + row0 + sub, :] = a[:, c * LANES:(c + 1) * LANES]

        def conv_block(nb):
            for c in range(nb * glu_cols // LANES, (nb + 1) * glu_cols // LANES):
                lanes = slice(c * LANES, (c + 1) * LANES)
                for r0 in range(row0, row0 + sub, CONV_ROWS):
                    acc = cdwb_ref[:, lanes] + cdw_ref[0:1, lanes] * abuf[
                        c, pl.ds(r0 + base, CONV_ROWS, stride=1), :]
                    for j in range(1, conv_w):
                        acc = acc + cdw_ref[j:j + 1, lanes] * abuf[
                            c, pl.ds(r0 + base + j, CONV_ROWS, stride=1), :]
                    cbuf[r0:r0 + CONV_ROWS, lanes] = acc

        def proj_qk():
            qk = _dot(vals["hb"], wqkvg_ref[:, :2 * dk_total])
            q_s[rows, :] = (qk[:, :dk_total] * (dk ** -0.5)).astype(jnp.bfloat16)
            k_s[rows, :] = qk[:, dk_total:]

        def proj_v():
            v_s[rows, :] = _dot(
                vals["hb"], wqkvg_ref[:, 2 * dk_total:2 * dk_total + dv_total]).astype(jnp.bfloat16)

        def proj_g():
            g_s[rows, :] = _dot(vals["hb"], wqkvg_ref[:, 2 * dk_total + dv_total:])

        def decay_low():
            vals["g_low"] = _dot(vals["hb"], wglow_ref[...])

        def decay_gate():
            gk = _dot(vals["g_low"].astype(jnp.bfloat16), wgk2_ref[...]) + bgk_ref[...]
            log_a = (jnp.minimum(gk, 0.0) - jnp.log1p(jnp.exp(-jnp.abs(gk)))) * (1.0 / GATE_TEMP)
            la_hi = log_a.astype(jnp.bfloat16)
            vals["hi"] = la_hi
            vals["lo"] = (log_a - la_hi.astype(jnp.float32)).astype(jnp.bfloat16)

        def decay_cumsum():
            tri = tri_ref[...]
            cum_s[rows, :] = _dot(tri, vals["hi"]) + _dot(tri, vals["lo"])

        def proj_merge():
            gate_s[rows, :] = _dot(vals["hb"], wmerge_ref[...])

        side = [decay_low, proj_qk, proj_v, proj_g, proj_merge]
        pieces = [[norm, functools.partial(glu_block, 0)]]
        for nb in range(n_blk):
            piece = [functools.partial(glu_block, nb + 1)] if nb + 1 < n_blk else []
            piece.append(functools.partial(conv_block, nb))
            piece.extend(side[2 * nb:2 * nb + 2])
            pieces.append(piece)
        pieces[-1].extend(side[2 * n_blk:])
        return pieces, [decay_gate, decay_cumsum]

    def stage2(row0):
        rows = slice(row0, row0 + sub2)
        chunks = list(range(row0, row0 + sub2, CHUNK))
        vals = {}

        def layernorm():
            for r0 in range(row0, row0 + sub2, CONV_ROWS):
                acc = cbuf[r0:r0 + CONV_ROWS, :]
                mu = jnp.mean(acc, axis=-1, keepdims=True)
                xc = acc - mu
                y = xc * lax.rsqrt(jnp.mean(xc * xc, axis=-1, keepdims=True) + EPS)
                y = y * lng_ref[...] + lnb_ref[...]
                ca_s[r0:r0 + CONV_ROWS, :] = (y * _sigmoid(y)).astype(jnp.bfloat16)

        def conv_out():
            gate_a = _sigmoid(gate_s[rows, :d_conv] + bmerge_ref[:, :d_conv])
            merged_s[rows, :] = gate_a * _dot(ca_s[rows, :], wco_ref[...])

        def gla_update():
            e_tots, upds = [], {}
            for ci, r0 in enumerate(chunks):
                cum = cum_s[r0:r0 + CHUNK, :]
                total = cum[CHUNK - 1:CHUNK, :]
                k_dec = (k_s[r0:r0 + CHUNK, :] * jnp.exp(total - cum)).astype(jnp.bfloat16)
                e_tots.append(jnp.exp(total))
                for h in range(heads):
                    v_h = v_s[r0:r0 + CHUNK, h * dv:(h + 1) * dv]
                    upds[ci, h] = lax.dot_general(v_h, k_dec[:, h * dk:(h + 1) * dk],
                                                  (((0,), (0,)), ((), ())),
                                                  preferred_element_type=jnp.float32)
            for h in range(heads):
                st = state[h]
                for ci in range(len(chunks)):
                    st = st * e_tots[ci][:, h * dk:(h + 1) * dk] + upds[ci, h]
                    vals[ci, h] = st.astype(jnp.bfloat16)
                state[h] = st

        def gla_readout():
            for ci, r0 in enumerate(chunks):
                for h in range(heads):
                    vs = slice(h * dv, (h + 1) * dv)
                    o_h = lax.dot_general(q_s[r0:r0 + CHUNK, h * dk:(h + 1) * dk], vals[ci, h],
                                          (((1,), (1,)), ((), ())),
                                          preferred_element_type=jnp.float32)
                    o_h = _rmsnorm(o_h, gnorm_ref[h:h + 1, :])
                    g_out = g_s[r0:r0 + CHUNK, vs]
                    ob_s[r0:r0 + CHUNK, vs] = (o_h * (g_out * _sigmoid(g_out))).astype(jnp.bfloat16)

        def output():
            gate_b = _sigmoid(gate_s[rows, d_conv:] + bmerge_ref[:, d_conv:])
            merged = merged_s[rows, :] + gate_b * _dot(ob_s[rows, :], wgo_ref[...])
            o_ref[rows, :] = x_ref[rows, :] + _dot(merged.astype(jnp.bfloat16), wout_ref[...])

        return [[layernorm], [conv_out], [gla_update], [gla_readout], [output]]

    late = []
    for row0 in range(0, tt, sub):
        pieces, late_fns = stage1(row0)
        late.extend(late_fns)
        for piece in pieces:
            for fn in piece:
                fn()
    for fn in late:
        fn()
    stage2_tiles = [stage2(row0) for row0 in range(0, tt, sub2)]
    for step in zip(*stage2_tiles):
        for piece in step:
            for fn in piece:
                fn()


def _mixer_call(x, nmix, wglu, wqkvg, wglow, wmerge, bmerge, cdw, cdwb, lng, lnb, wco,
                wgk2, bgk, gnorm, wgo, wout, tri):
    bsz, seq, d = x.shape
    tt, sub, sub2 = MIX_TT, MIX_SUB, MIX_SUB2
    conv_w, d_conv = cdw.shape
    dk_total = wgk2.shape[1]
    dv_total = wgo.shape[0]
    halo = _round_up(conv_w - 1, SUBLANES)
    assert seq % tt == 0 and tt % sub == 0 and sub % CHUNK == 0 and sub % CONV_ROWS == 0
    assert tt % sub2 == 0 and sub2 % CHUNK == 0 and sub2 % CONV_ROWS == 0
    assert halo <= tt and d_conv == d and d_conv % GLU_COLS == 0 and tri.shape == (sub, sub)

    kern = functools.partial(_mixer_kernel, tt=tt, sub=sub, sub2=sub2, halo=halo, conv_w=conv_w,
                             d_conv=d_conv, dk_total=dk_total, dv_total=dv_total)
    params = (nmix, wglu, wqkvg, wglow, wmerge, bmerge, cdw, cdwb, lng, lnb, wco, wgk2, bgk,
              gnorm, wgo, wout, tri)
    x_spec = pl.BlockSpec((None, tt, d), lambda b, t: (b, t, 0))
    f32, bf16 = jnp.float32, jnp.bfloat16
    return pl.pallas_call(
        kern,
        grid=(bsz, seq // tt),
        in_specs=[x_spec] + [_resident(p) for p in params],
        out_specs=x_spec,
        out_shape=jax.ShapeDtypeStruct(x.shape, x.dtype),
        scratch_shapes=[
            pltpu.VMEM((d_conv // LANES, halo + tt, LANES), f32),
            pltpu.VMEM((tt, d_conv), f32),
            pltpu.VMEM((tt, d_conv), bf16),
            pltpu.VMEM((tt, dk_total), bf16),
            pltpu.VMEM((tt, dk_total), f32),
            pltpu.VMEM((tt, dv_total), bf16),
            pltpu.VMEM((tt, dv_total), f32),
            pltpu.VMEM((tt, dk_total), f32),
            pltpu.VMEM((tt, 2 * d), f32),
            pltpu.VMEM((tt, d), f32),
            pltpu.VMEM((tt, dv_total), bf16),
            pltpu.VMEM((GLA_HEADS, dv_total // GLA_HEADS, dk_total // GLA_HEADS), f32),
        ],
        compiler_params=pltpu.CompilerParams(
            dimension_semantics=("arbitrary", "arbitrary"),
            vmem_limit_bytes=MIX_VMEM_BYTES),
        name="mixer",
    )(x, *params)


def _ffn_kernel(x_ref, nffn_ref, wup_ref, fdw_ref, fdwb_ref, wdown_ref, nfin_ref,
                o_ref, upbuf, act_s, *, tt, halo, conv_w, d_ff, col_blk, final_norm):
    t = pl.program_id(1)

    @pl.when(t == 0)
    def _():
        upbuf[:, 0:halo, :] = jnp.zeros((2 * d_ff // LANES, halo, LANES), jnp.float32)

    @pl.when(t > 0)
    def _():
        upbuf[:, 0:halo, :] = upbuf[:, tt:tt + halo, :]

    x = x_ref[...]
    hb = _rmsnorm(x, nffn_ref[...]).astype(jnp.bfloat16)
    base = halo - (conv_w - 1)

    def conv(col0):
        up = _dot(hb, wup_ref[:, col0:col0 + col_blk])
        slabs = []
        for c in range(col_blk // LANES):
            s = col0 // LANES + c
            lanes = slice(col0 + c * LANES, col0 + (c + 1) * LANES)
            upbuf[s, halo:halo + tt, :] = up[:, c * LANES:(c + 1) * LANES]
            acc = fdwb_ref[:, lanes] + fdw_ref[0:1, lanes] * upbuf[s, pl.ds(base, tt, stride=1), :]
            for j in range(1, conv_w):
                acc = acc + fdw_ref[j:j + 1, lanes] * upbuf[s, pl.ds(base + j, tt, stride=1), :]
            slabs.append(acc)
        return jnp.concatenate(slabs, axis=-1)

    for nb in range(d_ff // col_blk):
        u_gate = conv(nb * col_blk)
        u_val = conv(d_ff + nb * col_blk)
        act_s[:, nb * col_blk:(nb + 1) * col_blk] = (
            u_gate * _sigmoid(u_gate) * u_val).astype(jnp.bfloat16)

    y = x + _dot(act_s[...], wdown_ref[...])
    if final_norm:
        y = _rmsnorm(y, nfin_ref[...])
    o_ref[...] = y


def _ffn_call(x, nffn, wup, fdw, fdwb, wdown, nfin, final_norm):
    bsz, seq, d = x.shape
    tt = FFN_TT
    conv_w = fdw.shape[0]
    d_ff = wdown.shape[0]
    halo = _round_up(conv_w - 1, SUBLANES)
    col_blk = MXU_DIM
    assert seq % tt == 0 and d_ff % col_blk == 0 and halo <= tt

    kern = functools.partial(_ffn_kernel, tt=tt, halo=halo, conv_w=conv_w, d_ff=d_ff,
                             col_blk=col_blk, final_norm=final_norm)
    params = (nffn, wup, fdw, fdwb, wdown, nfin)
    x_spec = pl.BlockSpec((None, tt, d), lambda b, t: (b, t, 0))
    return pl.pallas_call(
        kern,
        grid=(bsz, seq // tt),
        in_specs=[x_spec] + [_resident(p) for p in params],
        out_specs=x_spec,
        out_shape=jax.ShapeDtypeStruct(x.shape, x.dtype),
        scratch_shapes=[
            pltpu.VMEM((2 * d_ff // LANES, halo + tt, LANES), jnp.float32),
            pltpu.VMEM((tt, d_ff), jnp.bfloat16),
        ],
        compiler_params=pltpu.CompilerParams(
            dimension_semantics=("arbitrary", "arbitrary"),
            vmem_limit_bytes=FFN_VMEM_BYTES),
        name="ffn",
    )(x, *params)


def _chunk_tri(n):
    r = jnp.arange(n)
    same_chunk = (r[:, None] // CHUNK) == (r[None, :] // CHUNK)
    return (same_chunk & (r[None, :] <= r[:, None])).astype(jnp.bfloat16)


def kernel(x, norm_mix, w_in, b_merge, conv_dw, conv_dw_b, conv_ln_g, conv_ln_b, w_conv_out,
           w_gk2, b_gk, gla_norm, w_gla_out, w_out, norm_ffn, w_up, ffn_dw, ffn_dw_b, w_down,
           norm_final):
    depth = w_in.shape[0]
    d = x.shape[-1]
    d_conv = conv_dw.shape[-1]
    rank, dk_total = w_gk2.shape[1], w_gk2.shape[2]
    dv_total = w_gla_out.shape[1]
    bf16 = jnp.bfloat16
    row = lambda v: v.reshape(1, -1)

    o_glu = 2 * d_conv
    o_qkvg = o_glu + 2 * dk_total + 2 * dv_total
    o_low = o_qkvg + rank
    rank_pad = _round_up(rank, LANES)
    tri = _chunk_tri(MIX_SUB)

    for l in range(depth):
        wl = w_in[l]
        wglow = jnp.pad(wl[:, o_qkvg:o_low], ((0, 0), (0, rank_pad - rank))).astype(bf16)
        wgk2 = jnp.pad(w_gk2[l], ((0, rank_pad - rank), (0, 0))).astype(bf16)
        x = _mixer_call(
            x, row(norm_mix[l]), wl[:, :o_glu].astype(bf16), wl[:, o_glu:o_qkvg].astype(bf16),
            wglow, wl[:, o_low:].astype(bf16), row(b_merge[l]), conv_dw[l], row(conv_dw_b[l]),
            row(conv_ln_g[l]), row(conv_ln_b[l]), w_conv_out[l].astype(bf16), wgk2,
            row(b_gk[l]), gla_norm[l], w_gla_out[l].astype(bf16), w_out[l].astype(bf16), tri)
        x = _ffn_call(
            x, row(norm_ffn[l]), w_up[l].astype(bf16), ffn_dw[l], row(ffn_dw_b[l]),
            w_down[l].astype(bf16), row(norm_final), final_norm=(l == depth - 1))
    return x
```

```python
import functools
import math

import jax
import jax.numpy as jnp
from jax import lax
from jax.experimental import pallas as pl
from jax.experimental.pallas import tpu as pltpu

CHUNK = 64
GLA_HEADS = 4
GATE_TEMP = 16.0
EPS = 1e-6

SUBLANES = 8
LANES = 128
MXU_DIM = 256
WIDE_COLS = 1024

MIX_TT = 512
MIX_SUB = 512
MIX_SUB2 = 256
FFN_TT = 512
CONV_ROWS = 32
GLU_COLS = MXU_DIM
MIX_VMEM_BYTES = 56 * 1024 * 1024
FFN_VMEM_BYTES = 56 * 1024 * 1024

_NEG_LOG2E = -1.0 / math.log(2.0)


def _round_up(n, m):
    return (n + m - 1) // m * m


def _dot(a, b):
    return jnp.dot(a, b, preferred_element_type=jnp.float32)


def _sigmoid(x):
    return 1.0 / (1.0 + jnp.exp2(x * _NEG_LOG2E))


def _rmsnorm(x, g):
    return x * lax.rsqrt(jnp.mean(x * x, axis=-1, keepdims=True) + EPS) * g


def _resident(arr):
    nd = arr.ndim
    return pl.BlockSpec(arr.shape, lambda *_: (0,) * nd, pipeline_mode=pl.Buffered(1))


def _mixer_kernel(x_ref, nmix_ref, wglu_ref, wqkvg_ref, wglow_ref, wmerge_ref, bmerge_ref,
                  cdw_ref, cdwb_ref, lng_ref, lnb_ref, wco_ref, wgk2_ref, bgk_ref,
                  gnorm_ref, wgo_ref, wout_ref, tri_ref,
                  o_ref,
                  abuf, cbuf, ca_s, q_s, k_s, v_s, g_s, cum_s, gate_s, merged_s, ob_s, state,
                  *, tt, sub, sub2, halo, conv_w, d_conv, dk_total, dv_total):
    heads = GLA_HEADS
    n_slab = d_conv // LANES
    dk = dk_total // heads
    dv = dv_total // heads
    t = pl.program_id(1)

    @pl.when(t == 0)
    def _():
        abuf[:, 0:halo, :] = jnp.zeros((n_slab, halo, LANES), jnp.float32)
        state[...] = jnp.zeros_like(state)

    @pl.when(t > 0)
    def _():
        abuf[:, 0:halo, :] = abuf[:, tt:tt + halo, :]

    base = halo - (conv_w - 1)
    glu_cols = GLU_COLS
    n_blk = d_conv // glu_cols

    def stage1(row0):
        rows = slice(row0, row0 + sub)
        vals = {}

        def norm():
            vals["hb"] = _rmsnorm(x_ref[rows, :], nmix_ref[...]).astype(jnp.bfloat16)

        def glu_block(nb):
            lo = nb * glu_cols
            val = _dot(vals["hb"], wglu_ref[:, lo:lo + glu_cols])
            gate = _dot(vals["hb"], wglu_ref[:, d_conv + lo:d_conv + lo + glu_cols])
            a = val * _sigmoid(gate)
            for c in range(glu_cols // LANES):
                abuf[lo // LANES + c, halo + row0:halo + row0 + sub, :] = a[:, c * LANES:(c + 1) * LANES]

        def conv_block(nb):
            for c in range(nb * glu_cols // LANES, (nb + 1) * glu_cols // LANES):
                lanes = slice(c * LANES, (c + 1) * LANES)
                for r0 in range(row0, row0 + sub, CONV_ROWS):
                    acc = cdwb_ref[:, lanes] + cdw_ref[0:1, lanes] * abuf[
                        c, pl.ds(r0 + base, CONV_ROWS, stride=1), :]
                    for j in range(1, conv_w):
                        acc = acc + cdw_ref[j:j + 1, lanes] * abuf[
                            c, pl.ds(r0 + base + j, CONV_ROWS, stride=1), :]
                    cbuf[r0:r0 + CONV_ROWS, lanes] = acc

        def proj_qk():
            qk = _dot(vals["hb"], wqkvg_ref[:, :2 * dk_total])
            q_s[rows, :] = (qk[:, :dk_total] * (dk ** -0.5)).astype(jnp.bfloat16)
            k_s[rows, :] = qk[:, dk_total:]

        def proj_v():
            v_s[rows, :] = _dot(
                vals["hb"], wqkvg_ref[:, 2 * dk_total:2 * dk_total + dv_total]).astype(jnp.bfloat16)

        def proj_g():
            g_s[rows, :] = _dot(
                vals["hb"], wqkvg_ref[:, 2 * dk_total + dv_total:2 * dk_total + 2 * dv_total])

        def decay_low():
            vals["g_low"] = _dot(vals["hb"], wglow_ref[...])

        def decay_gate():
            gk = _dot(vals["g_low"].astype(jnp.bfloat16), wgk2_ref[...]) + bgk_ref[...]
            log_a = (jnp.minimum(gk, 0.0) - jnp.log1p(jnp.exp(-jnp.abs(gk)))) * (1.0 / GATE_TEMP)
            la_hi = log_a.astype(jnp.bfloat16)
            vals["hi"] = la_hi
            vals["lo"] = (log_a - la_hi.astype(jnp.float32)).astype(jnp.bfloat16)

        def decay_cumsum():
            tri = tri_ref[...]
            cum_s[rows, :] = _dot(tri, vals["hi"]) + _dot(tri, vals["lo"])

        def proj_merge():
            gate_s[rows, :] = _dot(vals["hb"], wmerge_ref[:, :2 * d_conv])

        side = [decay_low, proj_qk, proj_v, proj_g, proj_merge]
        pieces = [[norm, functools.partial(glu_block, 0)]]
        for nb in range(n_blk):
            piece = [functools.partial(glu_block, nb + 1)] if nb + 1 < n_blk else []
            piece.append(functools.partial(conv_block, nb))
            piece.extend(side[2 * nb:2 * nb + 2])
            pieces.append(piece)
        pieces[-1].extend(side[2 * n_blk:])
        return pieces, [decay_gate, decay_cumsum]

    def stage2(row0):
        rows = slice(row0, row0 + sub2)
        chunks = list(range(row0, row0 + sub2, CHUNK))
        vals = {}

        def layernorm():
            for r0 in range(row0, row0 + sub2, CONV_ROWS):
                acc = cbuf[r0:r0 + CONV_ROWS, :]
                mu = jnp.mean(acc, axis=-1, keepdims=True)
                xc = acc - mu
                y = xc * lax.rsqrt(jnp.mean(xc * xc, axis=-1, keepdims=True) + EPS)
                y = y * lng_ref[...] + lnb_ref[...]
                ca_s[r0:r0 + CONV_ROWS, :] = (y * _sigmoid(y)).astype(jnp.bfloat16)

        def conv_out():
            gate_a = _sigmoid(gate_s[rows, :d_conv] + bmerge_ref[:, :d_conv])
            merged_s[rows, :] = gate_a * _dot(ca_s[rows, :], wco_ref[:, :d_conv])

        def gla_update():
            e_tots, upds = [], {}
            for ci, r0 in enumerate(chunks):
                cum = cum_s[r0:r0 + CHUNK, :]
                total = cum[CHUNK - 1:CHUNK, :]
                k_dec = (k_s[r0:r0 + CHUNK, :] * jnp.exp(total - cum)).astype(jnp.bfloat16)
                e_tots.append(jnp.exp(total))
                for h in range(heads):
                    v_h = v_s[r0:r0 + CHUNK, h * dv:(h + 1) * dv]
                    upds[ci, h] = lax.dot_general(v_h, k_dec[:, h * dk:(h + 1) * dk],
                                                  (((0,), (0,)), ((), ())),
                                                  preferred_element_type=jnp.float32)
            for h in range(heads):
                st = state[h]
                for ci in range(len(chunks)):
                    st = st * e_tots[ci][:, h * dk:(h + 1) * dk] + upds[ci, h]
                    vals[ci, h] = st.astype(jnp.bfloat16)
                state[h] = st

        def gla_readout():
            for ci, r0 in enumerate(chunks):
                for h in range(heads):
                    vs = slice(h * dv, (h + 1) * dv)
                    o_h = lax.dot_general(q_s[r0:r0 + CHUNK, h * dk:(h + 1) * dk], vals[ci, h],
                                          (((1,), (1,)), ((), ())),
                                          preferred_element_type=jnp.float32)
                    o_h = _rmsnorm(o_h, gnorm_ref[h:h + 1, :])
                    g_out = g_s[r0:r0 + CHUNK, vs]
                    ob_s[r0:r0 + CHUNK, vs] = (o_h * (g_out * _sigmoid(g_out))).astype(jnp.bfloat16)

        def gla_out():
            gate_b = _sigmoid(gate_s[rows, d_conv:] + bmerge_ref[:, d_conv:])
            merged = merged_s[rows, :] + gate_b * _dot(ob_s[rows, :], wgo_ref[:, :d_conv])
            vals["merged"] = merged.astype(jnp.bfloat16)

        def output():
            o_ref[rows, :] = x_ref[rows, :] + _dot(vals["merged"], wout_ref[:, :d_conv])

        return [[layernorm], [conv_out], [gla_update], [gla_readout], [gla_out], [output]]

    late = []
    for row0 in range(0, tt, sub):
        pieces, late_fns = stage1(row0)
        late.extend(late_fns)
        for piece in pieces:
            for fn in piece:
                fn()
    for fn in late:
        fn()
    stage2_tiles = [stage2(row0) for row0 in range(0, tt, sub2)]
    for step in zip(*stage2_tiles):
        for piece in step:
            for fn in piece:
                fn()


def _mixer_call(x, nmix, wglu, wqkvg, wglow, wmerge, bmerge, cdw, cdwb, lng, lnb, wco,
                wgk2, bgk, gnorm, wgo, wout, tri):
    bsz, seq, d = x.shape
    tt, sub, sub2 = MIX_TT, MIX_SUB, MIX_SUB2
    conv_w, d_conv = cdw.shape
    dk_total = wgk2.shape[1]
    dv_total = wgo.shape[0]
    halo = _round_up(conv_w - 1, SUBLANES)
    assert seq % tt == 0 and tt % sub == 0 and sub % CHUNK == 0 and sub % CONV_ROWS == 0
    assert tt % sub2 == 0 and sub2 % CHUNK == 0 and sub2 % CONV_ROWS == 0
    assert halo <= tt and d_conv == d and d_conv % GLU_COLS == 0 and tri.shape == (sub, sub)

    kern = functools.partial(_mixer_kernel, tt=tt, sub=sub, sub2=sub2, halo=halo, conv_w=conv_w,
                             d_conv=d_conv, dk_total=dk_total, dv_total=dv_total)
    params = (nmix, wglu, wqkvg, wglow, wmerge, bmerge, cdw, cdwb, lng, lnb, wco, wgk2, bgk,
              gnorm, wgo, wout, tri)
    x_spec = pl.BlockSpec((None, tt, d), lambda b, t: (b, t, 0))
    f32, bf16 = jnp.float32, jnp.bfloat16
    return pl.pallas_call(
        kern,
        grid=(bsz, seq // tt),
        in_specs=[x_spec] + [_resident(p) for p in params],
        out_specs=x_spec,
        out_shape=jax.ShapeDtypeStruct(x.shape, x.dtype),
        scratch_shapes=[
            pltpu.VMEM((d_conv // LANES, halo + tt, LANES), f32),
            pltpu.VMEM((tt, d_conv), f32),
            pltpu.VMEM((tt, d_conv), bf16),
            pltpu.VMEM((tt, dk_total), bf16),
            pltpu.VMEM((tt, dk_total), f32),
            pltpu.VMEM((tt, dv_total), bf16),
            pltpu.VMEM((tt, dv_total), f32),
            pltpu.VMEM((tt, dk_total), f32),
            pltpu.VMEM((tt, 2 * d), f32),
            pltpu.VMEM((tt, d), f32),
            pltpu.VMEM((tt, dv_total), bf16),
            pltpu.VMEM((GLA_HEADS, dv_total // GLA_HEADS, dk_total // GLA_HEADS), f32),
        ],
        compiler_params=pltpu.CompilerParams(
            dimension_semantics=("arbitrary", "arbitrary"),
            vmem_limit_bytes=MIX_VMEM_BYTES),
        name="mixer",
    )(x, *params)


def _ffn_kernel(x_ref, nffn_ref, wup_ref, fdw_ref, fdwb_ref, wdown_ref, nfin_ref,
                o_ref, upbuf, act_s, *, tt, halo, conv_w, d_ff, col_blk, final_norm):
    t = pl.program_id(1)

    @pl.when(t == 0)
    def _():
        upbuf[:, 0:halo, :] = jnp.zeros((2 * d_ff // LANES, halo, LANES), jnp.float32)

    @pl.when(t > 0)
    def _():
        upbuf[:, 0:halo, :] = upbuf[:, tt:tt + halo, :]

    x = x_ref[...]
    hb = _rmsnorm(x, nffn_ref[...]).astype(jnp.bfloat16)
    base = halo - (conv_w - 1)

    def conv(col0):
        up = _dot(hb, wup_ref[:, col0:col0 + col_blk])
        slabs = []
        for c in range(col_blk // LANES):
            s = col0 // LANES + c
            lanes = slice(col0 + c * LANES, col0 + (c + 1) * LANES)
            upbuf[s, halo:halo + tt, :] = up[:, c * LANES:(c + 1) * LANES]
            acc = fdwb_ref[:, lanes] + fdw_ref[0:1, lanes] * upbuf[s, pl.ds(base, tt, stride=1), :]
            for j in range(1, conv_w):
                acc = acc + fdw_ref[j:j + 1, lanes] * upbuf[s, pl.ds(base + j, tt, stride=1), :]
            slabs.append(acc)
        return jnp.concatenate(slabs, axis=-1)

    for nb in range(d_ff // col_blk):
        u_gate = conv(nb * col_blk)
        u_val = conv(d_ff + nb * col_blk)
        act_s[:, nb * col_blk:(nb + 1) * col_blk] = (
            u_gate * _sigmoid(u_gate) * u_val).astype(jnp.bfloat16)

    y = x + _dot(act_s[...], wdown_ref[:, :x.shape[-1]])
    if final_norm:
        y = _rmsnorm(y, nfin_ref[...])
    o_ref[...] = y


def _ffn_call(x, nffn, wup, fdw, fdwb, wdown, nfin, final_norm):
    bsz, seq, d = x.shape
    tt = FFN_TT
    conv_w = fdw.shape[0]
    d_ff = wdown.shape[0]
    halo = _round_up(conv_w - 1, SUBLANES)
    col_blk = MXU_DIM
    assert seq % tt == 0 and d_ff % col_blk == 0 and halo <= tt

    kern = functools.partial(_ffn_kernel, tt=tt, halo=halo, conv_w=conv_w, d_ff=d_ff,
                             col_blk=col_blk, final_norm=final_norm)
    params = (nffn, wup, fdw, fdwb, wdown, nfin)
    x_spec = pl.BlockSpec((None, tt, d), lambda b, t: (b, t, 0))
    return pl.pallas_call(
        kern,
        grid=(bsz, seq // tt),
        in_specs=[x_spec] + [_resident(p) for p in params],
        out_specs=x_spec,
        out_shape=jax.ShapeDtypeStruct(x.shape, x.dtype),
        scratch_shapes=[
            pltpu.VMEM((2 * d_ff // LANES, halo + tt, LANES), jnp.float32),
            pltpu.VMEM((tt, d_ff), jnp.bfloat16),
        ],
        compiler_params=pltpu.CompilerParams(
            dimension_semantics=("arbitrary", "arbitrary"),
            vmem_limit_bytes=FFN_VMEM_BYTES),
        name="ffn",
    )(x, *params)


def _chunk_tri(n):
    r = jnp.arange(n)
    same_chunk = (r[:, None] // CHUNK) == (r[None, :] // CHUNK)
    return (same_chunk & (r[None, :] <= r[:, None])).astype(jnp.bfloat16)


def kernel(x, norm_mix, w_in, b_merge, conv_dw, conv_dw_b, conv_ln_g, conv_ln_b, w_conv_out,
           w_gk2, b_gk, gla_norm, w_gla_out, w_out, norm_ffn, w_up, ffn_dw, ffn_dw_b, w_down,
           norm_final):
    depth = w_in.shape[0]
    d = x.shape[-1]
    d_conv = conv_dw.shape[-1]
    rank, dk_total = w_gk2.shape[1], w_gk2.shape[2]
    dv_total = w_gla_out.shape[1]
    row = lambda v: v.reshape(1, -1)

    def bf16(w):
        w = w.astype(jnp.bfloat16)
        if w.shape[-1] % WIDE_COLS == 0:
            w = jnp.pad(w, ((0, 0), (0, LANES)))
        return w

    o_glu = 2 * d_conv
    o_qkvg = o_glu + 2 * dk_total + 2 * dv_total
    o_low = o_qkvg + rank
    rank_pad = _round_up(rank, LANES)
    tri = _chunk_tri(MIX_SUB)

    for l in range(depth):
        wl = w_in[l]
        wglow = bf16(jnp.pad(wl[:, o_qkvg:o_low], ((0, 0), (0, rank_pad - rank))))
        wgk2 = bf16(jnp.pad(w_gk2[l], ((0, rank_pad - rank), (0, 0))))
        x = _mixer_call(
            x, row(norm_mix[l]), bf16(wl[:, :o_glu]), bf16(wl[:, o_glu:o_qkvg]),
            wglow, bf16(wl[:, o_low:]), row(b_merge[l]), conv_dw[l], row(conv_dw_b[l]),
            row(conv_ln_g[l]), row(conv_ln_b[l]), bf16(w_conv_out[l]), wgk2,
            row(b_gk[l]), gla_norm[l], bf16(w_gla_out[l]), bf16(w_out[l]), tri)
        x = _ffn_call(
            x, row(norm_ffn[l]), bf16(w_up[l]), ffn_dw[l], row(ffn_dw_b[l]),
            bf16(w_down[l]), row(norm_final), final_norm=(l == depth - 1))
    return x
```
